```python
import math
import jax, jax.numpy as jnp
from jax import lax
import numpy as np

D_MODEL = 1024
BATCH = 1
SEQ = 16384
DEPTH = 2
DEC_BATCH = 2
DEC_SEQ = 8192
PAST_LEN = 128

HEAD_DIM = 64
GLA_H = 4
GLA_DK = 32
GLA_DV = 64
GLA_LR = 16
GLA_TAU = 16.0
GLA_CHUNK = 64
SSD_H = 4
SSD_P = 64
SSD_G = 2
SSD_R = SSD_H // SSD_G
SSD_N = 128
SSD_CONV = 5
SSD_CHUNK = 128
SSD_DIN = SSD_H * SSD_P
SSD_CONV_CH = SSD_DIN + 2 * SSD_G * SSD_N
SWA_HQ = 4
SWA_HKV = 2
SWA_HALF = 128
SWA_BLOCK = 128
DIL_GROUPS = ((128, 1), (512, 4), (2048, 16))
N_DIL = 3
DIL_H = 4
DIL_BLOCK = 64
CROSS_H = 4
CROSS_DH = 128
MEM_LEN = 256
N_EXPERTS = 16
EC_FACTOR = 2
EXPERT_FF = 2048

EPS = 1e-6
NEG = -1e30

GLA_COLS = (GLA_H * GLA_DK, GLA_H * GLA_DK, GLA_H * GLA_DV, GLA_H * GLA_DV, GLA_LR, GLA_LR)
SSD_COLS = (SSD_DIN, SSD_CONV_CH, 2 * SSD_H)
SWA_COLS = (SWA_HQ * HEAD_DIM, SWA_HKV * HEAD_DIM, SWA_HKV * HEAD_DIM)
DIL_COLS = (N_DIL * DIL_H * HEAD_DIM, N_DIL * DIL_H * HEAD_DIM, N_DIL * DIL_H * HEAD_DIM)
IN_COLS = GLA_COLS + SSD_COLS + SWA_COLS + DIL_COLS
IN_WIDTH = sum(IN_COLS)
MIX_WIDTH = GLA_H * GLA_DV + SSD_DIN + SWA_HQ * HEAD_DIM + DIL_H * HEAD_DIM

kernel_name = "hybrid_parallel_bidir_encoder"


def rmsnorm(x, g):
    xf = x.astype(jnp.float32)
    y = xf * lax.rsqrt(jnp.mean(xf * xf, axis=-1, keepdims=True) + EPS)
    return (y * g.astype(jnp.float32)).astype(x.dtype)


def split_cols(t, sizes):
    out, start = [], 0
    for s in sizes:
        out.append(t[..., start:start + s])
        start += s
    return out


def alibi_slopes(n):
    return jnp.exp2(-8.0 * jnp.arange(1, n + 1, dtype=jnp.float32) / n)


def flip_t(t):
    return jnp.flip(t, axis=1)


def gla_chunked(q, k, v, lg, strict):
    B, T, H, K = q.shape
    V = v.shape[-1]
    L = GLA_CHUNK
    n = T // L
    c = lambda t: t.reshape(B, n, L, H, t.shape[-1])
    q, k, v, lg = c(q), c(k), c(v), c(lg)
    b = jnp.cumsum(lg, axis=2)
    b_end = b[:, :, -1:]
    q_dec = q * jnp.exp(b)
    k_inv = k * jnp.exp(-b)
    k_end = k * jnp.exp(b_end - b)
    mask = jnp.tril(jnp.ones((L, L), dtype=bool), -1 if strict else 0)
    att = jnp.where(mask, jnp.einsum('bnihk,bnjhk->bnhij', q_dec, k_inv), 0.0)
    o = jnp.einsum('bnhij,bnjhv->bnihv', att, v)
    s_chunk = jnp.einsum('bnjhk,bnjhv->nbhkv', k_end, v)
    decay = jnp.moveaxis(jnp.exp(b_end[:, :, 0]), 1, 0)

    def step(s, inp):
        d, sc = inp
        return d[..., None] * s + sc, s

    _, s_prev = lax.scan(step, jnp.zeros((B, H, K, V), jnp.float32), (decay, s_chunk))
    o = o + jnp.einsum('bnihk,nbhkv->bnihv', q_dec, s_prev)
    return o.reshape(B, T, H, V)


def gla_mixer(q, k, v, r, zf, zb, w_gate_up, b_gate, g_norm):
    dtype = q.dtype
    B, T, _ = q.shape
    f32 = jnp.float32
    q = q.reshape(B, T, GLA_H, GLA_DK).astype(f32) * (GLA_DK ** -0.5)
    k = k.reshape(B, T, GLA_H, GLA_DK).astype(f32)
    v = v.reshape(B, T, GLA_H, GLA_DV).astype(f32)

    def log_gate(z, w, b):
        a = jnp.einsum('btr,rk->btk', z.astype(f32), w.astype(f32)) + b.astype(f32)
        return (jax.nn.log_sigmoid(a) / GLA_TAU).reshape(B, T, GLA_H, GLA_DK)

    lg_f = log_gate(zf, w_gate_up[0], b_gate[0])
    lg_b = log_gate(zb, w_gate_up[1], b_gate[1])
    o_f = gla_chunked(q, k, v, lg_f, False)
    o_b = flip_t(gla_chunked(flip_t(q), flip_t(k), flip_t(v), flip_t(lg_b), True))
    o = o_f + o_b
    o = o * lax.rsqrt(jnp.mean(o * o, axis=-1, keepdims=True) + EPS) * g_norm.astype(f32)
    o = o.reshape(B, T, GLA_H * GLA_DV) * jax.nn.silu(r.astype(f32))
    return o.astype(dtype)


def centred_depthwise_conv(x, w, b):
    C = x.shape[-1]
    W = w.shape[0]
    y = lax.conv_general_dilated(x, w[:, None, :].astype(x.dtype), window_strides=(1,),
                                 padding=[(W // 2, W // 2)],
                                 dimension_numbers=('NWC', 'WIO', 'NWC'),
                                 feature_group_count=C)
    return y + b.astype(x.dtype)


def ssd_chunked(x, dt, A, Bm, Cm, strict):
    B, T, G, R, P = x.shape
    N = Bm.shape[-1]
    L = SSD_CHUNK
    n = T // L
    x = x.reshape(B, n, L, G, R, P)
    dt = dt.reshape(B, n, L, G, R)
    Bm = Bm.reshape(B, n, L, G, N)
    Cm = Cm.reshape(B, n, L, G, N)
    b = jnp.cumsum(dt * A, axis=2)
    b_end = b[:, :, -1]
    bt = jnp.moveaxis(b, 2, -1)
    dtt = jnp.moveaxis(dt, 2, -1)
    seg = bt[..., :, None] - bt[..., None, :]
    mask = jnp.tril(jnp.ones((L, L), dtype=bool), -1 if strict else 0)
    decay_ij = jnp.where(mask, jnp.exp(jnp.where(mask, seg, 0.0)), 0.0)
    cb = jnp.einsum('bnigs,bnjgs->bngij', Cm, Bm)
    w = cb[:, :, :, None] * decay_ij * dtt[..., None, :]
    y = jnp.einsum('bngrij,bnjgrp->bnigrp', w, x)
    xw = x * (jnp.exp(b_end[:, :, None] - b) * dt)[..., None]
    s_chunk = jnp.einsum('bnjgs,bnjgrp->nbgrsp', Bm, xw)
    decay = jnp.moveaxis(jnp.exp(b_end), 1, 0)

    def step(s, inp):
        d, sc = inp
        return d[..., None, None] * s + sc, s

    _, s_prev = lax.scan(step, jnp.zeros((B, G, R, N, P), jnp.float32), (decay, s_chunk))
    y = y + jnp.einsum('bnigs,nbgrsp->bnigrp', Cm, s_prev) * jnp.exp(b)[..., None]
    return y.reshape(B, T, G, R, P)


def ssd_mixer(z, xbc, dt_raw, conv_w, conv_b, dt_bias, a_log, d_skip, g_norm):
    dtype = z.dtype
    B, T, _ = z.shape
    f32 = jnp.float32
    xbc = jax.nn.silu(centred_depthwise_conv(xbc, conv_w, conv_b))
    xs, Bm, Cm = split_cols(xbc, (SSD_DIN, SSD_G * SSD_N, SSD_G * SSD_N))
    x = xs.reshape(B, T, SSD_G, SSD_R, SSD_P).astype(f32)
    Bm = Bm.reshape(B, T, SSD_G, SSD_N).astype(f32)
    Cm = Cm.reshape(B, T, SSD_G, SSD_N).astype(f32)
    dt = jax.nn.softplus(dt_raw.astype(f32).reshape(B, T, 2, SSD_G, SSD_R)
                         + dt_bias.astype(f32).reshape(1, 1, 2, SSD_G, SSD_R))
    A = -jnp.exp(a_log.astype(f32)).reshape(2, SSD_G, SSD_R)
    y_f = ssd_chunked(x, dt[:, :, 0], A[0], Bm, Cm, False)
    y_b = flip_t(ssd_chunked(flip_t(x), flip_t(dt[:, :, 1]), A[1], flip_t(Bm), flip_t(Cm), True))
    y = y_f + y_b + x * d_skip.astype(f32).reshape(SSD_G, SSD_R)[..., None]
    y = y.reshape(B, T, SSD_DIN) * jax.nn.silu(z.astype(f32))
    yg = y.reshape(B, T, SSD_G, SSD_DIN // SSD_G)
    yg = yg * lax.rsqrt(jnp.mean(yg * yg, axis=-1, keepdims=True) + EPS)
    return (yg.reshape(B, T, SSD_DIN) * g_norm.astype(f32)).astype(dtype)


def banded_attention(q, k, v, half, block, slopes, dist_scale, sink):
    f32 = jnp.float32
    B, T, Hk, G, Dh = q.shape
    nb = -(-T // block)
    Tp = nb * block
    q = jnp.pad(q, ((0, 0), (0, Tp - T), (0, 0), (0, 0), (0, 0)))
    kv_pad = ((0, 0), (block, Tp - T + block), (0, 0), (0, 0))

    def windows(t):
        t = jnp.pad(t, kv_pad).reshape(B, nb + 2, block, Hk, Dh)
        return jnp.concatenate([t[:, :-2], t[:, 1:-1], t[:, 2:]], axis=2)

    kw, vw = windows(k), windows(v)
    qb = q.reshape(B, nb, block, Hk, G, Dh)
    s = jnp.einsum('bnihgd,bnjhd->bnhgij', qb, kw, preferred_element_type=f32) * (Dh ** -0.5)
    rel = jnp.arange(3 * block)[None, :] - block - jnp.arange(block)[:, None]
    kpos = jnp.arange(nb)[:, None] * block - block + jnp.arange(3 * block)[None, :]
    valid = (jnp.abs(rel) <= half)[None] & ((kpos >= 0) & (kpos < T))[:, None, :]
    bias = -slopes.astype(f32)[:, :, None, None] * (jnp.abs(rel) * dist_scale).astype(f32)
    s = jnp.where(valid[None, :, None, None], s + bias, NEG)
    m = jnp.max(s, axis=-1)
    if sink is not None:
        sk = sink.astype(f32)[:, :, None]
        m = jnp.maximum(m, sk)
    p = jnp.exp(s - m[..., None])
    denom = jnp.sum(p, axis=-1)
    if sink is not None:
        denom = denom + jnp.exp(sk - m)
    o = jnp.einsum('bnhgij,bnjhd->bnihgd', p.astype(v.dtype), vw, preferred_element_type=f32)
    o = o / jnp.moveaxis(denom, -1, 2)[..., None]
    lse = jnp.moveaxis(m + jnp.log(denom), -1, 2)
    o = o.reshape(B, Tp, Hk, G, Dh)[:, :T]
    lse = lse.reshape(B, Tp, Hk, G)[:, :T]
    return o, lse


def swa_mixer(q, k, v, sink):
    B, T, _ = q.shape
    G = SWA_HQ // SWA_HKV
    q = q.reshape(B, T, SWA_HKV, G, HEAD_DIM)
    k = k.reshape(B, T, SWA_HKV, HEAD_DIM)
    v = v.reshape(B, T, SWA_HKV, HEAD_DIM)
    slopes = alibi_slopes(SWA_HQ).reshape(SWA_HKV, G)
    o, _ = banded_attention(q, k, v, SWA_HALF, SWA_BLOCK, slopes, 1, sink.reshape(SWA_HKV, G))
    return o.reshape(B, T, SWA_HQ * HEAD_DIM).astype(q.dtype)


def dilate(t, d):
    B, T = t.shape[:2]
    t = t.reshape(B, T // d, d, *t.shape[2:])
    return jnp.moveaxis(t, 2, 1).reshape(B * d, T // d, *t.shape[3:])


def undilate(t, B, d):
    Td = t.shape[1]
    t = t.reshape(B, d, Td, *t.shape[2:])
    return jnp.moveaxis(t, 1, 2).reshape(B, Td * d, *t.shape[3:])


def dilated_mixer(q, k, v):
    B, T, _ = q.shape
    shp = (B, T, N_DIL, DIL_H, HEAD_DIM)
    q, k, v = q.reshape(shp), k.reshape(shp), v.reshape(shp)
    slopes = alibi_slopes(N_DIL * DIL_H).reshape(N_DIL, DIL_H, 1)
    outs, lses = [], []
    for g, (window, d) in enumerate(DIL_GROUPS):
        qg = dilate(q[:, :, g, :, None, :], d)
        kg = dilate(k[:, :, g], d)
        vg = dilate(v[:, :, g], d)
        o, lse = banded_attention(qg, kg, vg, window // (2 * d), DIL_BLOCK, slopes[g], d, None)
        outs.append(undilate(o, B, d))
        lses.append(undilate(lse, B, d))
    o = jnp.stack(outs)
    alpha = jax.nn.softmax(jnp.stack(lses), axis=0)
    merged = jnp.sum(alpha[..., None] * o, axis=0)
    return merged.reshape(B, T, DIL_H * HEAD_DIM).astype(q.dtype)


def cross_attention(h, mem_n, w_q, w_kv, w_o):
    B, T, _ = h.shape
    M = mem_n.shape[1]
    q = (h @ w_q).reshape(B, T, CROSS_H, CROSS_DH)
    k, v = split_cols(mem_n @ w_kv, (CROSS_H * CROSS_DH, CROSS_H * CROSS_DH))
    k = k.reshape(B, M, CROSS_H, CROSS_DH)
    v = v.reshape(B, M, CROSS_H, CROSS_DH)
    s = jnp.einsum('bthd,bmhd->bhtm', q, k, preferred_element_type=jnp.float32) * (CROSS_DH ** -0.5)
    p = jax.nn.softmax(s, axis=-1)
    o = jnp.einsum('bhtm,bmhd->bthd', p.astype(v.dtype), v)
    return o.reshape(B, T, CROSS_H * CROSS_DH) @ w_o


def expert_choice_ffn(h, w_router, w_gate, w_up, w_down):
    B, T, Dm = h.shape
    n = B * T
    cap = EC_FACTOR * n // N_EXPERTS
    xf = h.reshape(n, Dm)
    aff = jax.nn.softmax(jnp.einsum('nd,de->ne', xf, w_router, preferred_element_type=jnp.float32), axis=-1)
    gate, idx = lax.top_k(aff.T, cap)
    xe = xf[idx]
    hid = jax.nn.silu(jnp.einsum('ecd,edf->ecf', xe, w_gate)) * jnp.einsum('ecd,edf->ecf', xe, w_up)
    ye = jnp.einsum('ecf,efd->ecd', hid, w_down) * gate[..., None].astype(h.dtype)
    y = jnp.zeros_like(xf).at[idx.reshape(-1)].add(ye.reshape(-1, Dm))
    return y.reshape(B, T, Dm)


def encoder_layer(x, mem, p, l):
    h = rmsnorm(x, p['norm_mix'][l])
    (gq, gk, gv, gr, gzf, gzb, sz, sxbc, sdt, cq, ck, cv, dq, dk, dv) = split_cols(h @ p['w_in'][l], IN_COLS)
    o_gla = gla_mixer(gq, gk, gv, gr, gzf, gzb, p['gla_w_gate_up'][l], p['gla_b_gate'][l], p['gla_norm'][l])
    o_ssd = ssd_mixer(sz, sxbc, sdt, p['ssd_conv_w'][l], p['ssd_conv_b'][l], p['ssd_dt_bias'][l],
                      p['ssd_a_log'][l], p['ssd_d'][l], p['ssd_norm'][l])
    o_swa = swa_mixer(cq, ck, cv, p['swa_sink'][l])
    o_dil = dilated_mixer(dq, dk, dv)
    mix = jnp.concatenate([o_gla, o_ssd, o_swa, o_dil], axis=-1).astype(x.dtype)
    x = x + mix @ p['w_out'][l]
    h = rmsnorm(x, p['norm_cross'][l])
    mem_n = rmsnorm(mem, p['norm_mem'][l])
    x = x + cross_attention(h, mem_n, p['w_cq'][l], p['w_ckv'][l], p['w_co'][l])
    h = rmsnorm(x, p['norm_ffn'][l])
    x = x + expert_choice_ffn(h, p['w_router'][l], p['w_e_gate'][l], p['w_e_up'][l], p['w_e_down'][l])
    return x


def run_trunk(x, mem, p):
    for l in range(DEPTH):
        x = encoder_layer(x, mem, p, l)
    return rmsnorm(x, p['final_norm'])


def setup_inputs(seed: int = 0) -> dict:
    key = jax.random.key(seed)
    ks = iter(jax.random.split(key, 40))
    nrm = lambda shape, scale: jax.random.normal(next(ks), shape, jnp.float32) * scale
    gain = lambda shape: 1.0 + 0.02 * jax.random.normal(next(ks), shape, jnp.float32)
    dt0 = jnp.exp(jax.random.uniform(next(ks), (DEPTH, 2, SSD_H), jnp.float32,
                                     math.log(1e-3), math.log(1e-1)))
    return {
        'x_prompt': nrm((BATCH, SEQ, D_MODEL), 1.0),
        'x_sample': nrm((DEC_BATCH, DEC_SEQ, D_MODEL), 1.0),
        'mem_prompt': nrm((BATCH, MEM_LEN, D_MODEL), 1.0),
        'mem_sample': nrm((DEC_BATCH, MEM_LEN, D_MODEL), 1.0),
        'norm_mix': gain((DEPTH, D_MODEL)),
        'w_in': nrm((DEPTH, D_MODEL, IN_WIDTH), D_MODEL ** -0.5),
        'gla_w_gate_up': nrm((DEPTH, 2, GLA_LR, GLA_H * GLA_DK), GLA_LR ** -0.5),
        'gla_b_gate': nrm((DEPTH, 2, GLA_H * GLA_DK), 0.1),
        'gla_norm': gain((DEPTH, GLA_DV)),
        'ssd_conv_w': nrm((DEPTH, SSD_CONV, SSD_CONV_CH), SSD_CONV ** -0.5),
        'ssd_conv_b': nrm((DEPTH, SSD_CONV_CH), 0.02),
        'ssd_dt_bias': dt0 + jnp.log(-jnp.expm1(-dt0)),
        'ssd_a_log': jnp.log(jax.random.uniform(next(ks), (DEPTH, 2, SSD_H), jnp.float32, 1.0, 16.0)),
        'ssd_d': 1.0 + nrm((DEPTH, SSD_H), 0.1),
        'ssd_norm': gain((DEPTH, SSD_DIN)),
        'swa_sink': nrm((DEPTH, SWA_HQ), 0.5),
        'w_out': nrm((DEPTH, MIX_WIDTH, D_MODEL), MIX_WIDTH ** -0.5),
        'norm_cross': gain((DEPTH, D_MODEL)),
        'norm_mem': gain((DEPTH, D_MODEL)),
        'w_cq': nrm((DEPTH, D_MODEL, CROSS_H * CROSS_DH), D_MODEL ** -0.5),
        'w_ckv': nrm((DEPTH, D_MODEL, 2 * CROSS_H * CROSS_DH), D_MODEL ** -0.5),
        'w_co': nrm((DEPTH, CROSS_H * CROSS_DH, D_MODEL), (CROSS_H * CROSS_DH) ** -0.5),
        'norm_ffn': gain((DEPTH, D_MODEL)),
        'w_router': nrm((DEPTH, D_MODEL, N_EXPERTS), D_MODEL ** -0.5),
        'w_e_gate': nrm((DEPTH, N_EXPERTS, D_MODEL, EXPERT_FF), D_MODEL ** -0.5),
        'w_e_up': nrm((DEPTH, N_EXPERTS, D_MODEL, EXPERT_FF), D_MODEL ** -0.5),
        'w_e_down': nrm((DEPTH, N_EXPERTS, EXPERT_FF, D_MODEL), EXPERT_FF ** -0.5),
        'final_norm': gain((D_MODEL,)),
    }


def reference(x_prompt, x_sample, mem_prompt, mem_sample, norm_mix, w_in, gla_w_gate_up, gla_b_gate,
              gla_norm, ssd_conv_w, ssd_conv_b, ssd_dt_bias, ssd_a_log, ssd_d, ssd_norm, swa_sink,
              w_out, norm_cross, norm_mem, w_cq, w_ckv, w_co, norm_ffn, w_router, w_e_gate, w_e_up,
              w_e_down, final_norm):
    params = dict(norm_mix=norm_mix, w_in=w_in, gla_w_gate_up=gla_w_gate_up, gla_b_gate=gla_b_gate,
                  gla_norm=gla_norm, ssd_conv_w=ssd_conv_w, ssd_conv_b=ssd_conv_b, ssd_dt_bias=ssd_dt_bias,
                  ssd_a_log=ssd_a_log, ssd_d=ssd_d, ssd_norm=ssd_norm, swa_sink=swa_sink, w_out=w_out,
                  norm_cross=norm_cross, norm_mem=norm_mem, w_cq=w_cq, w_ckv=w_ckv, w_co=w_co,
                  norm_ffn=norm_ffn, w_router=w_router, w_e_gate=w_e_gate, w_e_up=w_e_up,
                  w_e_down=w_e_down, final_norm=final_norm)
    y_prompt = run_trunk(x_prompt, mem_prompt, params)
    y_sample = run_trunk(x_sample, mem_sample, params)
    return (y_prompt, y_sample)
```

```python
import functools
import math

import jax
import jax.numpy as jnp
from jax import lax
from jax.experimental import pallas as pl
from jax.experimental.pallas import tpu as pltpu

F32 = jnp.float32
BF16 = jnp.bfloat16
I32 = jnp.int32

D_MODEL = 1024
DEPTH = 2
EPS = 1e-6
NEG = -1e30

GLA_H, GLA_DK, GLA_DV, GLA_LR, GLA_TAU, GLA_CHUNK = 4, 32, 64, 16, 16.0, 64
SSD_H, SSD_P, SSD_G, SSD_N, SSD_CONV, SSD_CHUNK = 4, 64, 2, 128, 5, 128
SSD_DIN = SSD_H * SSD_P
HEAD_DIM = 64
SWA_HQ, SWA_HKV, SWA_HALF = 4, 2, 128
DIL_GROUPS = ((128, 1), (512, 4), (2048, 16))
DIL_H = 4
CROSS_H, CROSS_DH, MEM_LEN = 4, 128, 256
N_EXPERTS, EC_FACTOR, EXPERT_FF = 16, 2, 2048

GLA_W = 896
SSD_W = 1024
SWA_W = 512
DIL_W = 256

VMEM_LIMIT = 56 * 1024 * 1024

TM_PROJ = 512
TB_SCAN = 512
QB_ATT = 256
SB_ATT = 128
TT_SEL = 128
CH_FFN = 512
TT_CMB = 256
CK_CMB = 128


def _cparams(sem):
    return pltpu.CompilerParams(dimension_semantics=sem, vmem_limit_bytes=VMEM_LIMIT)


def _split2(x):
    hi = x.astype(BF16)
    lo = (x - hi.astype(F32)).astype(BF16)
    return hi, lo


def _dot(a, b):
    return jnp.dot(a, b, preferred_element_type=F32)


def _dot_nt(a, b):
    return lax.dot_general(a, b, (((1,), (1,)), ((), ())), preferred_element_type=F32)


def _dot_tn(a, b):
    return lax.dot_general(a, b, (((0,), (0,)), ((), ())), preferred_element_type=F32)


def _dot_exact_l(m, x):
    hi, lo = _split2(x)
    return _dot(m, hi) + _dot(m, lo)


def _dot_exact_r(x, m):
    hi, lo = _split2(x)
    return _dot(hi, m) + _dot(lo, m)


def _dot3(a, b):
    ah, al = _split2(a)
    bh, bl = _split2(b)
    return _dot(ah, bh) + _dot(al, bh) + _dot(ah, bl)


def _dot3_nt(a, b):
    ah, al = _split2(a)
    bh, bl = _split2(b)
    return _dot_nt(ah, bh) + _dot_nt(al, bh) + _dot_nt(ah, bl)


def _rms(x, g):
    return x * lax.rsqrt(jnp.mean(x * x, axis=-1, keepdims=True) + EPS) * g


def _silu(x):
    return x * (1.0 / (1.0 + jnp.exp(-x)))


def _log_sigmoid(a):
    return jnp.minimum(a, 0.0) - jnp.log1p(jnp.exp(-jnp.abs(a)))


def _softplus(a):
    return jnp.maximum(a, 0.0) + jnp.log1p(jnp.exp(-jnp.abs(a)))


def _is_in(i, values):
    out = None
    for v in values:
        c = i == v
        out = c if out is None else (out | c)
    return out


def _proj_in_kernel(x_ref, g_ref, w_ref, wdt_ref, *refs):
    out_refs, stage_ref = refs[:-1], refs[-1]
    tm = x_ref.shape[0]
    h = _rms(x_ref[...], g_ref[...]).astype(BF16)
    off = 0
    for ref in out_refs[:-1]:
        d = tm // ref.shape[0]
        w = ref.shape[1] // d
        res = _dot(h, w_ref[:, off:off + w])
        if d == 1:
            ref[...] = res.astype(ref.dtype)
        else:
            for c in range(w // 128):
                stage_ref[c] = res[:, c * 128:(c + 1) * 128]
            for r in range(d):
                for c in range(w // 128):
                    ref[:, r * w + c * 128:r * w + (c + 1) * 128] = stage_ref[
                        c, pl.ds(r, tm // d, stride=d), :].astype(ref.dtype)
        off += w
    out_refs[-1][...] = _dot_nt(wdt_ref[...], h)


def proj_in(x, g, w, wdt):
    n = x.shape[0]
    tm = min(TM_PROJ, n)
    outs = [(GLA_W, F32, 1), (SSD_W, F32, 1), (SWA_W, BF16, 1)]
    for _, d in DIL_GROUPS:
        outs += [(DIL_W, BF16, d)] * 3
    out_shape = [jax.ShapeDtypeStruct((n // d, d * wd), dt) for wd, dt, d in outs]
    out_shape.append(jax.ShapeDtypeStruct((8, n), F32))
    out_specs = [pl.BlockSpec((tm // d, d * wd), lambda i: (i, 0)) for wd, _, d in outs]
    out_specs.append(pl.BlockSpec((8, tm), lambda i: (0, i)))
    return pl.pallas_call(
        _proj_in_kernel,
        grid=(n // tm,),
        in_specs=[pl.BlockSpec((tm, D_MODEL), lambda i: (i, 0)),
                  pl.BlockSpec((1, D_MODEL), lambda i: (0, 0)),
                  pl.BlockSpec(w.shape, lambda i: (0, 0)),
                  pl.BlockSpec(wdt.shape, lambda i: (0, 0))],
        out_specs=out_specs,
        out_shape=out_shape,
        scratch_shapes=[pltpu.VMEM((DIL_W // 128, tm, 128), F32)],
        compiler_params=_cparams(("parallel",)),
        name="proj_in",
    )(x, g, w, wdt)


def _gla_kernel(*refs, reverse, reset_blocks, nblk):
    if reverse:
        gla_ref, of_ref, wg_ref, wgt_ref, bg_ref, bgt_ref, gn_ref, out_ref, s_ref = refs
    else:
        gla_ref, wg_ref, wgt_ref, bg_ref, bgt_ref, out_ref, s_ref = refs
    tb = gla_ref.shape[0]
    nch = tb // GLA_CHUNK
    i = pl.program_id(0)
    blk = (nblk - 1 - i) if reverse else i

    @pl.when(_is_in(blk, reset_blocks))
    def _():
        s_ref[...] = jnp.zeros_like(s_ref)

    q = gla_ref[:, 0:128] * (GLA_DK ** -0.5)
    k = gla_ref[:, 128:256]
    v = gla_ref[:, 256:512]
    zpad = gla_ref[:, 768:896]

    row = lax.broadcasted_iota(I32, (tb, tb), 0)
    col = lax.broadcasted_iota(I32, (tb, tb), 1)
    same = (row >> 6) == (col >> 6)
    if reverse:
        cum_m = same & (col >= row)
        att_m = same & (col > row)
    else:
        cum_m = same & (col <= row)
        att_m = cum_m
    cum_mat = cum_m.astype(BF16)
    same_mat = same.astype(BF16)

    lg = _log_sigmoid(_dot3(zpad, wg_ref[...]) + bg_ref[...]) * (1.0 / GLA_TAU)
    lgt = _log_sigmoid(_dot3_nt(wgt_ref[...], zpad) + bgt_ref[...]) * (1.0 / GLA_TAU)
    b = _dot_exact_l(cum_mat, lg)
    bend = _dot_exact_l(same_mat, lg)
    bend_t = _dot_exact_r(lgt, same_mat)
    q_dec = q * jnp.exp(b)
    k_inv = (k * jnp.exp(-b)).astype(BF16)
    k_end = (k * jnp.exp(bend - b)).astype(BF16)
    vb = v.astype(BF16)

    lane_q = lax.broadcasted_iota(I32, (tb, 128), 1) >> 5
    lane_v = lax.broadcasted_iota(I32, (tb, 256), 1) >> 6

    o = jnp.zeros((tb, 256), F32)
    for h in range(GLA_H):
        qh = jnp.where(lane_q == h, q_dec, 0.0).astype(BF16)
        att = jnp.where(att_m, _dot_nt(qh, k_inv), 0.0).astype(BF16)
        vh = jnp.where(lane_v == h, v, 0.0).astype(BF16)
        o = o + _dot(att, vh)

    hk = lax.broadcasted_iota(I32, (128, 256), 0) >> 5
    hv = lax.broadcasted_iota(I32, (128, 256), 1) >> 6
    head_m = hk == hv
    s = s_ref[...]
    states = [None] * nch
    order = range(nch - 1, -1, -1) if reverse else range(nch)
    for c in order:
        states[c] = jnp.where(head_m, s, 0.0).astype(BF16)
        lo = c * GLA_CHUNK
        d_col = jnp.exp(bend_t[:, lo:lo + 1])
        u = _dot_tn(k_end[lo:lo + GLA_CHUNK], vb[lo:lo + GLA_CHUNK])
        s = d_col * s + u
    s_ref[...] = s
    chunk_of_row = lax.broadcasted_iota(I32, (tb, 128), 0) >> 6
    q_exp = jnp.concatenate(
        [jnp.where(chunk_of_row == c, q_dec, 0.0).astype(BF16) for c in range(nch)], axis=1)
    o = o + _dot(q_exp, jnp.concatenate(states, axis=0))

    if not reverse:
        out_ref[...] = o
    else:
        o = o + of_ref[...]
        r = gla_ref[:, 512:768]
        hm = ((lax.broadcasted_iota(I32, (256, 256), 0) >> 6)
              == (lax.broadcasted_iota(I32, (256, 256), 1) >> 6)).astype(BF16)
        ms = _dot_exact_r(o * o, hm) * (1.0 / GLA_DV)
        o = o * lax.rsqrt(ms + EPS) * gn_ref[...] * _silu(r)
        out_ref[...] = o.astype(out_ref.dtype)


def gla_params(w_gate_up, b_gate, g_norm):
    per_dir = []
    for d in range(2):
        wg = jnp.zeros((128, 128), F32).at[d * GLA_LR:(d + 1) * GLA_LR].set(w_gate_up[d].astype(F32))
        bg = b_gate[d].astype(F32)
        per_dir.append((wg, wg.T, bg[None, :], bg[:, None]))
    gn = jnp.tile(g_norm.astype(F32), GLA_H)[None, :]
    return per_dir[0], per_dir[1], gn


def gla_sweep(gla, wg, wgt, bg, bgt, bounds, reverse, o_f=None, gn=None):
    n = gla.shape[0]
    tb = TB_SCAN
    nblk = n // tb
    if reverse:
        reset = tuple(b // tb - 1 for b in bounds[1:])
        imap = lambda i: (nblk - 1 - i, 0)
    else:
        reset = tuple(b // tb for b in bounds[:-1])
        imap = lambda i: (i, 0)
    const = lambda i: (0, 0)
    ins = [gla]
    in_specs = [pl.BlockSpec((tb, GLA_W), imap)]
    if reverse:
        ins.append(o_f)
        in_specs.append(pl.BlockSpec((tb, 256), imap))
    ins += [wg, wgt, bg, bgt]
    in_specs += [pl.BlockSpec((128, 128), const), pl.BlockSpec((128, 128), const),
                 pl.BlockSpec((1, 128), const), pl.BlockSpec((128, 1), const)]
    if reverse:
        ins.append(gn)
        in_specs.append(pl.BlockSpec((1, 256), const))
    return pl.pallas_call(
        functools.partial(_gla_kernel, reverse=reverse, reset_blocks=reset, nblk=nblk),
        grid=(nblk,),
        in_specs=in_specs,
        out_specs=pl.BlockSpec((tb, 256), imap),
        out_shape=jax.ShapeDtypeStruct((n, 256), BF16 if reverse else F32),
        scratch_shapes=[pltpu.VMEM((128, 256), F32)],
        compiler_params=_cparams(("arbitrary",)),
        name="gla_bwd" if reverse else "gla_fwd",
    )(*ins)


def _ssd_kernel(*refs, reverse, start_blocks, end_blocks, nblk):
    if reverse:
        (cur_ref, prev_ref, next_ref, dtt_ref, yf_ref, cw_ref, cb_ref, dtb_ref, alog_ref,
         dsk_ref, gn_ref, out_ref, s_ref) = refs
    else:
        cur_ref, prev_ref, next_ref, dtt_ref, cw_ref, cb_ref, dtb_ref, alog_ref, out_ref, s_ref = refs
    tb = cur_ref.shape[0]
    L = SSD_CHUNK
    nch = tb // L
    d0 = 4 if reverse else 0
    i = pl.program_id(0)
    blk = (nblk - 1 - i) if reverse else i
    at_start = _is_in(blk, start_blocks)
    at_end = _is_in(blk, end_blocks)

    @pl.when(at_end if reverse else at_start)
    def _():
        s_ref[...] = jnp.zeros_like(s_ref)

    prev = jnp.where(at_start, 0.0, prev_ref[:, 256:1024])
    nxt = jnp.where(at_end, 0.0, next_ref[:, 256:1024])
    ext = jnp.concatenate([prev, cur_ref[:, 256:1024], nxt], axis=0)
    acc = jnp.zeros((tb, 768), F32) + cb_ref[...]
    for j in range(SSD_CONV):
        acc = acc + cw_ref[j:j + 1, :] * ext[6 + j:6 + j + tb]
    xbc = _silu(acc)
    xs = xbc[:, 0:256]
    bm = xbc[:, 256:512].astype(BF16)
    cm = xbc[:, 512:768].astype(BF16)

    dt_t = _softplus(dtt_ref[...] + dtb_ref[...])
    dta_t = dt_t * (-jnp.exp(alog_ref[...]))
    row = lax.broadcasted_iota(I32, (tb, tb), 0)
    col = lax.broadcasted_iota(I32, (tb, tb), 1)
    same = (row >> 7) == (col >> 7)
    if reverse:
        cum_m = same & (col >= row)
        cum_tm = same & (row >= col)
    else:
        cum_m = same & (col <= row)
        cum_tm = same & (row <= col)
    cum_mat = cum_m.astype(BF16)
    cum_tmat = cum_tm.astype(BF16)
    same_mat = same.astype(BF16)
    eye_mat = (row == col).astype(BF16)
    dta_hi, dta_lo = _split2(dta_t)
    dt_hi, dt_lo = _split2(dt_t)
    b_row = _dot(dta_hi, cum_tmat) + _dot(dta_lo, cum_tmat)
    b_col = _dot_nt(cum_mat, dta_hi) + _dot_nt(cum_mat, dta_lo)
    bend_col = _dot_nt(same_mat, dta_hi) + _dot_nt(same_mat, dta_lo)
    dt_col = _dot_nt(eye_mat, dt_hi) + _dot_nt(eye_mat, dt_lo)
    expand = ((lax.broadcasted_iota(I32, (8, 256), 1) >> 6) + d0
              == lax.broadcasted_iota(I32, (8, 256), 0)).astype(BF16)
    b256 = _dot_exact_r(b_col, expand)
    bend256 = _dot_exact_r(bend_col, expand)
    dt256 = _dot_exact_r(dt_col, expand)
    eb256 = jnp.exp(b256)
    xw = (xs * (jnp.exp(bend256 - b256) * dt256)).astype(BF16)
    dec256 = jnp.exp(bend256)

    lane_h = lax.broadcasted_iota(I32, (L, 256), 1) >> 6
    ri = lax.broadcasted_iota(I32, (L, L), 0)
    ci = lax.broadcasted_iota(I32, (L, L), 1)
    tri = (ci > ri) if reverse else (ci <= ri)

    s = s_ref[...]
    ys = [None] * nch
    order = range(nch - 1, -1, -1) if reverse else range(nch)
    for c in order:
        lo = c * L
        xs_c = xs[lo:lo + L]
        y = jnp.zeros((L, 256), F32)
        y_int = []
        s_new = []
        for g in range(SSD_G):
            c_g = cm[lo:lo + L, g * 128:(g + 1) * 128]
            b_g = bm[lo:lo + L, g * 128:(g + 1) * 128]
            cb = _dot_nt(c_g, b_g)
            for r in range(SSD_H // SSD_G):
                hh = g * (SSD_H // SSD_G) + r
                bi = b_col[lo:lo + L, d0 + hh:d0 + hh + 1]
                bj = b_row[d0 + hh:d0 + hh + 1, lo:lo + L]
                dec = jnp.where(tri, jnp.exp(jnp.where(tri, bi - bj, 0.0)), 0.0)
                w = (cb * dec * dt_t[d0 + hh:d0 + hh + 1, lo:lo + L]).astype(BF16)
                xh = jnp.where(lane_h == hh, xs_c, 0.0).astype(BF16)
                y = y + _dot(w, xh)
            s_g = s[:, g * 128:(g + 1) * 128]
            y_int.append(_dot(c_g, s_g.astype(BF16)))
            s_new.append(dec256[lo:lo + 1, g * 128:(g + 1) * 128] * s_g
                         + _dot_tn(b_g, xw[lo:lo + L, g * 128:(g + 1) * 128]))
        ys[c] = y + jnp.concatenate(y_int, axis=1) * eb256[lo:lo + L]
        s = jnp.concatenate(s_new, axis=1)
    s_ref[...] = s
    y = jnp.concatenate(ys, axis=0)

    if not reverse:
        out_ref[...] = y
    else:
        y = y + yf_ref[...] + xs * dsk_ref[...]
        y = y * _silu(cur_ref[:, 0:256])
        gm = ((lax.broadcasted_iota(I32, (256, 256), 0) >> 7)
              == (lax.broadcasted_iota(I32, (256, 256), 1) >> 7)).astype(BF16)
        ms = _dot_exact_r(y * y, gm) * (1.0 / (SSD_DIN // SSD_G))
        out_ref[...] = (y * lax.rsqrt(ms + EPS) * gn_ref[...]).astype(out_ref.dtype)


def ssd_params(conv_w, conv_b, dt_bias, a_log, d_skip, g_norm):
    return (conv_w.astype(F32), conv_b.astype(F32)[None, :],
            dt_bias.astype(F32).reshape(8, 1), a_log.astype(F32).reshape(8, 1),
            jnp.repeat(d_skip.astype(F32), SSD_P)[None, :], g_norm.astype(F32)[None, :])


def ssd_sweep(ssd, dtt, prm, bounds, reverse, y_f=None):
    cw, cb, dtb, alog, dsk, gn = prm
    n = ssd.shape[0]
    tb = TB_SCAN
    nblk = n // tb
    r8 = tb // 8
    nrow8 = n // 8
    start_blocks = tuple(b // tb for b in bounds[:-1])
    end_blocks = tuple(b // tb - 1 for b in bounds[1:])
    if reverse:
        bi = lambda i: nblk - 1 - i
    else:
        bi = lambda i: i
    const = lambda i: (0, 0)
    ins = [ssd, ssd, ssd, dtt]
    in_specs = [pl.BlockSpec((tb, SSD_W), lambda i: (bi(i), 0)),
                pl.BlockSpec((8, SSD_W), lambda i: (jnp.maximum(bi(i) * r8 - 1, 0), 0)),
                pl.BlockSpec((8, SSD_W), lambda i: (jnp.minimum((bi(i) + 1) * r8, nrow8 - 1), 0)),
                pl.BlockSpec((8, tb), lambda i: (0, bi(i)))]
    if reverse:
        ins.append(y_f)
        in_specs.append(pl.BlockSpec((tb, 256), lambda i: (bi(i), 0)))
    ins += [cw, cb, dtb, alog]
    in_specs += [pl.BlockSpec((SSD_CONV, 768), const), pl.BlockSpec((1, 768), const),
                 pl.BlockSpec((8, 1), const), pl.BlockSpec((8, 1), const)]
    if reverse:
        ins += [dsk, gn]
        in_specs += [pl.BlockSpec((1, 256), const), pl.BlockSpec((1, 256), const)]
    return pl.pallas_call(
        functools.partial(_ssd_kernel, reverse=reverse, start_blocks=start_blocks,
                          end_blocks=end_blocks, nblk=nblk),
        grid=(nblk,),
        in_specs=in_specs,
        out_specs=pl.BlockSpec((tb, 256), lambda i: (bi(i), 0)),
        out_shape=jax.ShapeDtypeStruct((n, 256), BF16 if reverse else F32),
        scratch_shapes=[pltpu.VMEM((SSD_N, 256), F32)],
        compiler_params=_cparams(("arbitrary",)),
        name="ssd_bwd" if reverse else "ssd_fwd",
    )(*ins)


def alibi_slopes(n):
    return [2.0 ** (-8.0 * (i + 1) / n) for i in range(n)]


def _band_kernel(*refs, hq, hkv, half, slopes, has_sink, with_lse, vbounds, qb):
    q_ref, kp_ref, kc_ref, kn_ref, vp_ref, vc_ref, vn_ref = refs[:7]
    rest = list(refs[7:])
    sink_ref = rest.pop(0) if has_sink else None
    o_ref = rest.pop(0)
    lse_ref = rest.pop(0) if with_lse else None
    start = pl.program_id(1) * qb
    seq_lo = jnp.int32(vbounds[0])
    seq_hi = jnp.int32(vbounds[-1])
    for b in vbounds[1:-1]:
        seq_lo = jnp.where(start >= b, b, seq_lo)
    for b in reversed(vbounds[1:-1]):
        seq_hi = jnp.where(start < b, b, seq_hi)
    kfull = jnp.concatenate([kp_ref[...], kc_ref[...], kn_ref[...]], axis=0)
    vfull = jnp.concatenate([vp_ref[...], vc_ref[...], vn_ref[...]], axis=0)
    grp = hq // hkv
    sb = SB_ATT
    win = sb + 2 * half
    rel = (lax.broadcasted_iota(I32, (sb, win), 1) - half) - lax.broadcasted_iota(I32, (sb, win), 0)
    absrel = jnp.abs(rel)
    absrel_f = absrel.astype(F32)
    kcol = lax.broadcasted_iota(I32, (sb, win), 1)
    for j in range(qb // sb):
        kpos = start + (j * sb - half) + kcol
        valid = (absrel <= half) & (kpos >= seq_lo) & (kpos < seq_hi)
        kwin = kfull[j * sb:j * sb + win]
        vwin = vfull[j * sb:j * sb + win]
        for h in range(hq):
            kvh = h // grp
            qh = q_ref[j * sb:(j + 1) * sb, h * HEAD_DIM:(h + 1) * HEAD_DIM]
            s = _dot_nt(qh, kwin[:, kvh * HEAD_DIM:(kvh + 1) * HEAD_DIM]) * (HEAD_DIM ** -0.5)
            s = jnp.where(valid, s - slopes[h] * absrel_f, NEG)
            m = jnp.max(s, axis=-1, keepdims=True)
            if has_sink:
                sk = sink_ref[:, h:h + 1]
                m = jnp.maximum(m, sk)
            p = jnp.exp(s - m)
            den = jnp.sum(p, axis=-1, keepdims=True)
            if has_sink:
                den = den + jnp.exp(sk - m)
            o = _dot(p.astype(BF16), vwin[:, kvh * HEAD_DIM:(kvh + 1) * HEAD_DIM]) / den
            o_ref[j * sb:(j + 1) * sb, h * HEAD_DIM:(h + 1) * HEAD_DIM] = o.astype(o_ref.dtype)
            if with_lse:
                lse_ref[j * sb:(j + 1) * sb, h * HEAD_DIM:(h + 1) * HEAD_DIM] = jnp.broadcast_to(
                    m + jnp.log(den), (sb, HEAD_DIM))


def band_attention(q_src, k_src, v_src, *, d, hq, hkv, half, slopes, bounds, sink=None,
                   with_lse=False, out_dtype=BF16):
    tv = q_src[0].shape[0]
    qb = min(QB_ATT, tv)
    nb = tv // qb
    rh = qb // half
    nh = tv // half
    vbounds = tuple(b // d for b in bounds)

    def view(src):
        arr, w, cb0 = src
        return arr, w, cb0, arr.shape[1] // d // w

    specs, ins = [], []
    qa, qw, qc0, qcb = view(q_src)
    ins.append(qa)
    specs.append(pl.BlockSpec((qb, qw), lambda r, i: (i, r * qcb + qc0)))
    for src in (k_src, v_src):
        a, w, c0, cbn = view(src)
        ins += [a, a, a]
        specs += [pl.BlockSpec((half, w), lambda r, i, c0=c0, cbn=cbn: (jnp.maximum(i * rh - 1, 0), r * cbn + c0)),
                  pl.BlockSpec((qb, w), lambda r, i, c0=c0, cbn=cbn: (i, r * cbn + c0)),
                  pl.BlockSpec((half, w), lambda r, i, c0=c0, cbn=cbn: (jnp.minimum((i + 1) * rh, nh - 1), r * cbn + c0))]
    if sink is not None:
        ins.append(sink)
        specs.append(pl.BlockSpec((1, hq), lambda r, i: (0, 0)))
    ow = hq * HEAD_DIM
    out_shape = [jax.ShapeDtypeStruct((tv, d * ow), out_dtype)]
    out_specs = [pl.BlockSpec((qb, ow), lambda r, i: (i, r))]
    if with_lse:
        out_shape.append(jax.ShapeDtypeStruct((tv, d * ow), F32))
        out_specs.append(pl.BlockSpec((qb, ow), lambda r, i: (i, r)))
    outs = pl.pallas_call(
        functools.partial(_band_kernel, hq=hq, hkv=hkv, half=half, slopes=tuple(slopes),
                          has_sink=sink is not None, with_lse=with_lse, vbounds=vbounds, qb=qb),
        grid=(d, nb),
        in_specs=specs,
        out_specs=out_specs,
        out_shape=out_shape,
        compiler_params=_cparams(("parallel", "parallel")),
        name="band_d%d" % d,
    )(*ins)
    return list(outs)


def _memkv_kernel(mem_ref, g_ref, w_ref, k_ref, v_ref):
    h = _rms(mem_ref[0], g_ref[...]).astype(BF16)
    kv = _dot(h, w_ref[...])
    half = CROSS_H * CROSS_DH
    k_ref[0] = kv[:, :half].astype(BF16)
    v_ref[0] = kv[:, half:].astype(BF16)


def mem_kv(mem, g, w):
    nb = mem.shape[0]
    half = CROSS_H * CROSS_DH
    return pl.pallas_call(
        _memkv_kernel,
        grid=(nb,),
        in_specs=[pl.BlockSpec((1, MEM_LEN, D_MODEL), lambda b: (b, 0, 0)),
                  pl.BlockSpec((1, D_MODEL), lambda b: (0, 0)),
                  pl.BlockSpec((D_MODEL, 2 * half), lambda b: (0, 0))],
        out_specs=[pl.BlockSpec((1, MEM_LEN, half), lambda b: (b, 0, 0))] * 2,
        out_shape=[jax.ShapeDtypeStruct((nb, MEM_LEN, half), BF16)] * 2,
        compiler_params=_cparams(("parallel",)),
        name="mem_kv",
    )(mem, g, w)


def _post_kernel(x_ref, gla_ref, ssd_ref, swa_ref, o1_ref, o4_ref, o16_ref, l1_ref, l4_ref, l16_ref,
                 wout_ref, gc_ref, wcq_ref, k_ref, v_ref, wco_ref, gf_ref, wrt_ref, x2_ref, aff_ref,
                 *stage_refs):
    tm = x_ref.shape[0]

    def token_major(ref, stage_ref):
        d = tm // ref.shape[0]
        nc = DIL_W // 128
        for r in range(d):
            for c in range(nc):
                stage_ref[c, pl.ds(r, tm // d, stride=d), :] = ref[:, r * DIL_W + c * 128:r * DIL_W + (c + 1) * 128]
        return jnp.concatenate([stage_ref[c] for c in range(nc)], axis=1)

    l1, o1 = l1_ref[...], o1_ref[...]
    o4, o16 = token_major(o4_ref, stage_refs[0]), token_major(o16_ref, stage_refs[1])
    l4, l16 = token_major(l4_ref, stage_refs[2]), token_major(l16_ref, stage_refs[3])
    m = jnp.maximum(jnp.maximum(l1, l4), l16)
    e1, e4, e16 = jnp.exp(l1 - m), jnp.exp(l4 - m), jnp.exp(l16 - m)
    dil = (e1 * o1 + e4 * o4 + e16 * o16) / (e1 + e4 + e16)
    x1 = (x_ref[...] + _dot(gla_ref[...], wout_ref[0:256, :]) + _dot(ssd_ref[...], wout_ref[256:512, :])
          + _dot(swa_ref[...], wout_ref[512:768, :]) + _dot(dil.astype(BF16), wout_ref[768:1024, :]))

    h = _rms(x1, gc_ref[...]).astype(BF16)
    q = (_dot(h, wcq_ref[...]) * (CROSS_DH ** -0.5)).astype(BF16)
    heads = []
    for hh in range(CROSS_H):
        c = slice(hh * CROSS_DH, (hh + 1) * CROSS_DH)
        s = _dot_nt(q[:, c], k_ref[0, :, c])
        s = s - jnp.max(s, axis=-1, keepdims=True)
        p = jnp.exp(s)
        p = p / jnp.sum(p, axis=-1, keepdims=True)
        heads.append(_dot(p.astype(BF16), v_ref[0, :, c]))
    x2 = x1 + _dot(jnp.concatenate(heads, axis=1).astype(BF16), wco_ref[...])
    x2_ref[...] = x2

    h3 = _rms(x2, gf_ref[...])
    logits = _dot3_nt(h3, wrt_ref[...])
    logits = logits - jnp.max(logits, axis=-1, keepdims=True)
    ex = jnp.exp(logits)
    aff_ref[...] = ex / jnp.sum(ex, axis=-1, keepdims=True)


def post_mix(x, o_gla, o_ssd, o_swa, dil_o, dil_l, w_out, g_cross, w_cq, k_mem, v_mem, w_co, g_ffn, w_rt,
             bounds):
    n = x.shape[0]
    tm = min(TM_PROJ, n)
    row = lambda i: (i, 0)
    const = lambda i: (0, 0)

    def mem_idx(i):
        s = i * tm
        r = 0
        for b in bounds[1:-1]:
            r = r + (s >= b).astype(I32)
        return (r, 0, 0)

    half = CROSS_H * CROSS_DH
    dil_specs = [pl.BlockSpec((tm // d, d * DIL_W), row) for _, d in DIL_GROUPS]
    in_specs = ([pl.BlockSpec((tm, D_MODEL), row)] + [pl.BlockSpec((tm, 256), row)] * 3 + dil_specs * 2
                + [pl.BlockSpec((D_MODEL, D_MODEL), const), pl.BlockSpec((1, D_MODEL), const),
                   pl.BlockSpec((D_MODEL, half), const),
                   pl.BlockSpec((1, MEM_LEN, half), mem_idx), pl.BlockSpec((1, MEM_LEN, half), mem_idx),
                   pl.BlockSpec((half, D_MODEL), const), pl.BlockSpec((1, D_MODEL), const),
                   pl.BlockSpec((N_EXPERTS, D_MODEL), const)])
    return pl.pallas_call(
        _post_kernel,
        grid=(n // tm,),
        in_specs=in_specs,
        out_specs=[pl.BlockSpec((tm, D_MODEL), row), pl.BlockSpec((tm, N_EXPERTS), row)],
        out_shape=[jax.ShapeDtypeStruct((n, D_MODEL), F32), jax.ShapeDtypeStruct((n, N_EXPERTS), F32)],
        scratch_shapes=[pltpu.VMEM((DIL_W // 128, tm, 128), F32)] * 4,
        compiler_params=_cparams(("parallel",)),
        name="post_mix",
    )(x, o_gla, o_ssd, o_swa, *dil_o, *dil_l, w_out, g_cross, w_cq, k_mem, v_mem, w_co, g_ffn, w_rt)


def _select_kernel(aff_ref, sel_ref, pos_ref, base_ref, afft_ref, *, cap):
    ng = aff_ref.shape[0]
    nblk = ng // TT_SEL
    nrc, _, rows = afft_ref.shape
    eye = (lax.broadcasted_iota(I32, (N_EXPERTS, N_EXPERTS), 0)
           == lax.broadcasted_iota(I32, (N_EXPERTS, N_EXPERTS), 1))
    eye_mat = eye.astype(BF16)

    def transpose(c, carry):
        x = aff_ref[pl.ds(pl.multiple_of(c * rows, rows), rows), :]
        p1 = x.astype(BF16)
        r1 = x - p1.astype(F32)
        p2 = r1.astype(BF16)
        p3 = (r1 - p2.astype(F32)).astype(BF16)
        afft_ref[c] = _dot_nt(eye_mat, p1) + _dot_nt(eye_mat, p2) + _dot_nt(eye_mat, p3)
        return carry
    lax.fori_loop(0, nrc, transpose, 0)

    def count_ge(th):
        def body(c, acc):
            return acc + jnp.sum(jnp.where(afft_ref[c] >= th, 1.0, 0.0), axis=1, keepdims=True)
        return lax.fori_loop(0, nrc, body, jnp.zeros((N_EXPERTS, 1), F32))

    def bisect(it, prefix):
        cand = prefix | jnp.left_shift(jnp.int32(1), 30 - it)
        take = count_ge(lax.bitcast_convert_type(cand, F32)) >= cap
        return jnp.where(take, cand, prefix)
    prefix = lax.fori_loop(0, 31, bisect, jnp.zeros((N_EXPERTS, 1), I32))
    lo = lax.bitcast_convert_type(prefix, F32)
    hi = lax.bitcast_convert_type(prefix + 1, F32)

    def refine(it, carry):
        lo, hi = carry
        mid = 0.5 * (lo + hi)
        take = count_ge(mid) >= cap
        return jnp.where(take, mid, lo), jnp.where(take, hi, mid)
    lo, hi = lax.fori_loop(0, 24, refine, (lo, hi))
    need = cap - count_ge(hi)

    def as_row(col):
        return jnp.sum(jnp.where(eye, col, 0.0), axis=0, keepdims=True)
    lo, hi, need = as_row(lo), as_row(hi), as_row(need)

    tri = (lax.broadcasted_iota(I32, (TT_SEL, TT_SEL), 1)
           < lax.broadcasted_iota(I32, (TT_SEL, TT_SEL), 0)).astype(BF16)

    def block(b, carry):
        base_sel, base_eq = carry
        r0 = pl.multiple_of(b * TT_SEL, TT_SEL)
        aff = aff_ref[pl.ds(r0, TT_SEL), :]
        sure = aff >= hi
        tie = (aff >= lo) & (aff < hi)
        eq = jnp.where(tie, 1.0, 0.0)
        eq_rank = _dot(tri, eq.astype(BF16)) + base_eq
        sel = jnp.where(sure | (tie & (eq_rank < need)), 1.0, 0.0)
        sel_ref[pl.ds(r0, TT_SEL), :] = sel
        pos_ref[pl.ds(r0, TT_SEL), :] = _dot(tri, sel.astype(BF16)).astype(I32)
        base_ref[b] = base_sel.astype(I32)
        return (base_sel + jnp.sum(sel, axis=0, keepdims=True),
                base_eq + jnp.sum(eq, axis=0, keepdims=True))

    zero = jnp.zeros((1, N_EXPERTS), F32)
    base_sel, _ = lax.fori_loop(0, nblk, block, (zero, zero))
    base_ref[nblk] = base_sel.astype(I32)


def select_tokens(aff, n_groups, cap):
    n = aff.shape[0]
    ng = n // n_groups
    nblk = ng // TT_SEL
    return pl.pallas_call(
        functools.partial(_select_kernel, cap=cap),
        grid=(n_groups,),
        in_specs=[pl.BlockSpec((ng, N_EXPERTS), lambda g: (g, 0))],
        out_specs=[pl.BlockSpec((ng, N_EXPERTS), lambda g: (g, 0)),
                   pl.BlockSpec((ng, N_EXPERTS), lambda g: (g, 0)),
                   pl.BlockSpec((None, nblk + 1, 1, N_EXPERTS), lambda g: (g, 0, 0, 0))],
        out_shape=[jax.ShapeDtypeStruct((n, N_EXPERTS), F32),
                   jax.ShapeDtypeStruct((n, N_EXPERTS), I32),
                   jax.ShapeDtypeStruct((n_groups, nblk + 1, 1, N_EXPERTS), I32)],
        scratch_shapes=[pltpu.VMEM((ng // 1024, N_EXPERTS, 1024), F32)],
        compiler_params=_cparams(("parallel",)),
        name="select",
    )(aff)


def _compact_kernel(base_ref, aff_ref, sel_ref, pos_ref, idx_ref, gate_ref, *, ng, nblk):
    g = pl.program_id(0)
    idx_ref[...] = jnp.zeros_like(idx_ref)
    gate_ref[...] = jnp.zeros_like(gate_ref)
    lane = lax.broadcasted_iota(I32, (TT_SEL, 128), 1)
    sub = lax.broadcasted_iota(I32, (TT_SEL, 128), 0)

    def block(b, carry):
        r0 = pl.multiple_of(b * TT_SEL, TT_SEL)
        aff = aff_ref[pl.ds(r0, TT_SEL), :]
        sel = sel_ref[pl.ds(r0, TT_SEL), :] > 0.5
        pos = pos_ref[pl.ds(r0, TT_SEL), :]
        tok = sub + (g * ng + b * TT_SEL)
        for e in range(N_EXPERTS):
            base = base_ref[(g * (nblk + 1) + b) * N_EXPERTS + e]
            w = base >> 7
            col = pos[:, e:e + 1] + (base & 127)
            sel_e = sel[:, e:e + 1]
            gate_e = aff[:, e:e + 1]
            for half in range(2):
                hit = sel_e & (col == lane + 128 * half)
                idx_ref[e, pl.ds(w + half, 1)] += jnp.sum(jnp.where(hit, tok, 0), axis=0, keepdims=True)[None]
                gate_ref[e, pl.ds(w + half, 1)] += jnp.sum(jnp.where(hit, gate_e, 0.0), axis=0,
                                                           keepdims=True)[None]
        return carry

    lax.fori_loop(0, nblk, block, 0)


def compact_slots(aff, sel, pos, base, n_groups, cap):
    n = aff.shape[0]
    ng = n // n_groups
    nblk = ng // TT_SEL
    nrow = cap // 128 + 1
    tok_spec = pl.BlockSpec((ng, N_EXPERTS), lambda g, b: (g, 0))
    out_spec = pl.BlockSpec((None, N_EXPERTS, nrow, 1, 128), lambda g, b: (g, 0, 0, 0, 0))
    return pl.pallas_call(
        functools.partial(_compact_kernel, ng=ng, nblk=nblk),
        grid_spec=pltpu.PrefetchScalarGridSpec(
            num_scalar_prefetch=1, grid=(n_groups,),
            in_specs=[tok_spec, tok_spec, tok_spec], out_specs=[out_spec, out_spec]),
        out_shape=[jax.ShapeDtypeStruct((n_groups, N_EXPERTS, nrow, 1, 128), I32),
                   jax.ShapeDtypeStruct((n_groups, N_EXPERTS, nrow, 1, 128), F32)],
        compiler_params=_cparams(("parallel",)),
        name="compact",
    )(base.reshape(-1), aff, sel, pos)


YE_W = D_MODEL + 128


def _ffn_kernel(idx_ref, idx_next_ref, x_hbm, gid_ref, gf_ref, wg_ref, wu_ref, wd_ref, out_ref,
                xbuf, sems, *, nsteps):
    ch = xbuf.shape[1]
    step = pl.program_id(0) * pl.num_programs(1) + pl.program_id(1)
    slot = step % 2

    def gather(ids_ref, sl, unroll):
        def body(r, carry):
            t = ids_ref[0, 0, r]
            pltpu.make_async_copy(x_hbm.at[pl.ds(t, 1), :], xbuf.at[sl, pl.ds(r, 1), :],
                                  sems.at[sl]).start()
            return carry
        lax.fori_loop(0, ch, body, 0, unroll=unroll)

    def wait_rows(sl):
        pltpu.make_async_copy(x_hbm.at[pl.ds(0, ch), :], xbuf.at[sl], sems.at[sl]).wait()

    @pl.when(step == 0)
    def _():
        gather(idx_ref, 0, 8)

    wait_rows(slot)
    h = _rms(xbuf[slot], gf_ref[...]).astype(BF16)
    gather(idx_next_ref, 1 - slot, True)

    ft = 512
    acc = jnp.zeros((ch, D_MODEL), F32)
    for f in range(EXPERT_FF // ft):
        gt = _dot(h, wg_ref[0, :, f * ft:(f + 1) * ft])
        up = _dot(h, wu_ref[0, :, f * ft:(f + 1) * ft])
        acc = acc + _dot((_silu(gt) * up).astype(BF16), wd_ref[0, f * ft:(f + 1) * ft, :])
    gid = gid_ref[0]
    out_ref[0, :, 0:D_MODEL] = (acc * gid[:, 0:1]).astype(out_ref.dtype)
    tok = gid[:, 1:2]
    tok_hi = jnp.floor(tok * (1.0 / 128.0))
    lane = lax.broadcasted_iota(I32, (ch, YE_W - D_MODEL), 1)
    out_ref[0, :, D_MODEL:YE_W] = jnp.where(lane == 0, tok_hi, jnp.where(lane == 1, tok - 128.0 * tok_hi, 0.0)
                                            ).astype(out_ref.dtype)

    @pl.when(step == nsteps - 1)
    def _():
        wait_rows(1 - slot)


def expert_ffn(x2, idx, gate, g_ffn, w_gate, w_up, w_down):
    n_e, r = idx.shape
    ch = CH_FFN
    nj = r // ch
    nsteps = n_e * nj
    idx3 = idx.reshape(nsteps, 1, ch)
    gid = jnp.stack([gate, idx.astype(F32)], axis=-1)
    lin = lambda e, j: e * nj + j
    return pl.pallas_call(
        functools.partial(_ffn_kernel, nsteps=nsteps),
        grid=(n_e, nj),
        in_specs=[pl.BlockSpec((1, 1, ch), lambda e, j: (lin(e, j), 0, 0), memory_space=pltpu.SMEM),
                  pl.BlockSpec((1, 1, ch), lambda e, j: (jnp.minimum(lin(e, j) + 1, nsteps - 1), 0, 0),
                               memory_space=pltpu.SMEM),
                  pl.BlockSpec(memory_space=pl.ANY),
                  pl.BlockSpec((1, ch, 2), lambda e, j: (e, j, 0)),
                  pl.BlockSpec((1, D_MODEL), lambda e, j: (0, 0)),
                  pl.BlockSpec((1, D_MODEL, EXPERT_FF), lambda e, j: (e, 0, 0)),
                  pl.BlockSpec((1, D_MODEL, EXPERT_FF), lambda e, j: (e, 0, 0)),
                  pl.BlockSpec((1, EXPERT_FF, D_MODEL), lambda e, j: (e, 0, 0))],
        out_specs=pl.BlockSpec((1, ch, YE_W), lambda e, j: (e, j, 0)),
        out_shape=jax.ShapeDtypeStruct((n_e, r, YE_W), BF16),
        scratch_shapes=[pltpu.VMEM((2, ch, D_MODEL), F32), pltpu.SemaphoreType.DMA((2,))],
        compiler_params=_cparams(("arbitrary", "arbitrary")),
        name="expert_ffn",
    )(idx3, idx3, x2, gid, g_ffn, w_gate, w_up, w_down)


PK_ALIGN = 16
PK_BLOCK = 1024


def _pow2_down(hi, lo):
    out, s = [], hi
    while s >= lo:
        out.append(s)
        s //= 2
    return out


def _combine_kernel(base_ref, x_ref, gfin_ref, ye_hbm, out_ref, buf, sems,
                    *, tiles_per_group, nblk, cap, ntiles, final):
    tt = x_ref.shape[0]
    i = pl.program_id(0)
    slot = i % 2
    piece_sizes = _pow2_down(tt, PK_ALIGN)
    max_rows = N_EXPERTS * (tt + PK_ALIGN)
    wait_top = min(1 << (max_rows.bit_length() - 1), 1 << (ye_hbm.shape[1].bit_length() - 1))
    wait_sizes = [wait_top] * (max_rows // wait_top - 1) + _pow2_down(wait_top, PK_ALIGN)

    def runs(tile):
        g = tile // tiles_per_group
        b0 = (tile % tiles_per_group) * (tt // TT_SEL)
        out = []
        for e in range(N_EXPERTS):
            lo = g * cap + base_ref[(g * (nblk + 1) + b0) * N_EXPERTS + e]
            hi = g * cap + base_ref[(g * (nblk + 1) + b0 + tt // TT_SEL) * N_EXPERTS + e]
            lo_al = (lo // PK_ALIGN) * PK_ALIGN
            hi_al = ((hi + PK_ALIGN - 1) // PK_ALIGN) * PK_ALIGN
            out.append((lo_al, jnp.where(hi > lo, hi_al - lo_al, 0)))
        return out

    def total_rows(tile):
        return functools.reduce(lambda a, b: a + b, [ln for _, ln in runs(tile)])

    def issue(tile, sl):
        dst = jnp.int32(0)
        for e, (src, ln) in enumerate(runs(tile)):
            rem = ln
            for s in piece_sizes:
                take = rem >= s

                @pl.when(take)
                def _():
                    pltpu.make_async_copy(
                        ye_hbm.at[e, pl.ds(pl.multiple_of(src, PK_ALIGN), s), :],
                        buf.at[sl, pl.ds(pl.multiple_of(dst, PK_ALIGN), s), :], sems.at[sl]).start()
                inc = jnp.where(take, s, 0)
                src, dst, rem = src + inc, dst + inc, rem - inc

    def wait_all(tile, sl):
        rem = total_rows(tile)
        for s in wait_sizes:
            take = rem >= s

            @pl.when(take)
            def _():
                pltpu.make_async_copy(ye_hbm.at[0, pl.ds(0, s), :], buf.at[sl, pl.ds(0, s), :],
                                      sems.at[sl]).wait()
            rem = rem - jnp.where(take, s, 0)

    @pl.when(i == 0)
    def _():
        buf[...] = jnp.zeros_like(buf)
        issue(i, 0)

    @pl.when(i + 1 < ntiles)
    def _():
        issue(i + 1, 1 - slot)

    wait_all(i, slot)
    total = total_rows(i)
    tok_row = (lax.broadcasted_iota(I32, (1, tt), 1) + i * tt).astype(F32)

    def scatter_block(r0):
        ids = buf[slot, pl.ds(r0, PK_BLOCK), D_MODEL:YE_W].astype(F32)
        tok = ids[:, 0:1] * 128.0 + ids[:, 1:2]
        live = lax.broadcasted_iota(I32, (PK_BLOCK, 1), 0) + r0 < total
        onehot_t = jnp.where(live & (tok == tok_row), 1.0, 0.0).astype(BF16)
        return _dot_tn(onehot_t, buf[slot, pl.ds(r0, PK_BLOCK), 0:D_MODEL])

    out_ref[...] = x_ref[...] + scatter_block(0)

    @pl.when(total > PK_BLOCK)
    def _():
        def more(b, carry):
            out_ref[...] += scatter_block(pl.multiple_of(b * PK_BLOCK, PK_BLOCK))
            return carry
        lax.fori_loop(1, (total + PK_BLOCK - 1) // PK_BLOCK, more, 0)

    if final:
        out_ref[...] = _rms(out_ref[...], gfin_ref[...])


def combine(x2, ye, base, g_final, n_groups, cap, final):
    n = x2.shape[0]
    tt = TT_CMB
    ntiles = n // tt
    nblk = (n // n_groups) // TT_SEL
    max_rows = N_EXPERTS * (tt + PK_ALIGN)
    buf_rows = -(-max_rows // PK_BLOCK) * PK_BLOCK
    return pl.pallas_call(
        functools.partial(_combine_kernel, tiles_per_group=ntiles // n_groups, nblk=nblk, cap=cap,
                          ntiles=ntiles, final=final),
        grid_spec=pltpu.PrefetchScalarGridSpec(
            num_scalar_prefetch=1, grid=(ntiles,),
            in_specs=[pl.BlockSpec((tt, D_MODEL), lambda i, b: (i, 0)),
                      pl.BlockSpec((1, D_MODEL), lambda i, b: (0, 0)),
                      pl.BlockSpec(memory_space=pl.ANY)],
            out_specs=pl.BlockSpec((tt, D_MODEL), lambda i, b: (i, 0)),
            scratch_shapes=[pltpu.VMEM((2, buf_rows, YE_W), BF16), pltpu.SemaphoreType.DMA((2,))]),
        out_shape=jax.ShapeDtypeStruct((n, D_MODEL), F32),
        compiler_params=_cparams(("arbitrary",)),
        name="combine",
    )(base.reshape(-1), x2, g_final, ye)


def _in_proj_weights(w_in_l):
    sizes = (128, 128, 256, 256, 16, 16, 256, 768, 8, 256, 128, 128, 768, 768, 768)
    parts, s = [], 0
    for sz in sizes:
        parts.append(w_in_l[:, s:s + sz])
        s += sz
    gq, gk, gv, gr, gzf, gzb, sz_, sxbc, sdt, cq, ck, cv, dq, dk, dv = parts
    pad = jnp.zeros((D_MODEL, GLA_W - 800), w_in_l.dtype)
    cols = [gq, gk, gv, gr, gzf, gzb, pad, sz_, sxbc, cq, ck, cv]
    for g in range(3):
        c = slice(g * 256, (g + 1) * 256)
        cols += [dq[:, c], dk[:, c], dv[:, c]]
    return jnp.concatenate(cols, axis=1).astype(BF16), sdt.T.astype(BF16)


def _layer(x, mem, p, l, bounds, n_groups, final_gain):
    n = x.shape[0]
    row = lambda a: a.astype(F32)[None, :]
    w_main, w_dt = _in_proj_weights(p['w_in'][l])
    outs = proj_in(x, row(p['norm_mix'][l]), w_main, w_dt)
    gla, ssd, swa = outs[0], outs[1], outs[2]
    dil = outs[3:12]
    dtt = outs[12]

    gp = gla_params(p['gla_w_gate_up'][l], p['gla_b_gate'][l], p['gla_norm'][l])
    o_gla = gla_sweep(gla, *gp[0], bounds, False)
    o_gla = gla_sweep(gla, *gp[1], bounds, True, o_f=o_gla, gn=gp[2])

    sp = ssd_params(p['ssd_conv_w'][l], p['ssd_conv_b'][l], p['ssd_dt_bias'][l], p['ssd_a_log'][l],
                    p['ssd_d'][l], p['ssd_norm'][l])
    o_ssd = ssd_sweep(ssd, dtt, sp, bounds, False)
    o_ssd = ssd_sweep(ssd, dtt, sp, bounds, True, y_f=o_ssd)

    (o_swa,) = band_attention((swa, 256, 0), (swa, 128, 2), (swa, 128, 3), d=1, hq=SWA_HQ, hkv=SWA_HKV,
                              half=SWA_HALF, slopes=alibi_slopes(SWA_HQ), bounds=bounds,
                              sink=row(p['swa_sink'][l]))
    sl = alibi_slopes(len(DIL_GROUPS) * DIL_H)
    dil_o, dil_l = [], []
    for g, (window, d) in enumerate(DIL_GROUPS):
        o, lse = band_attention((dil[3 * g], 256, 0), (dil[3 * g + 1], 256, 0), (dil[3 * g + 2], 256, 0),
                                d=d, hq=DIL_H, hkv=DIL_H, half=window // (2 * d),
                                slopes=[s * d for s in sl[g * DIL_H:(g + 1) * DIL_H]], bounds=bounds,
                                with_lse=True, out_dtype=F32)
        dil_o.append(o)
        dil_l.append(lse)

    k_mem, v_mem = mem_kv(mem, row(p['norm_mem'][l]), p['w_ckv'][l].astype(BF16))
    x2, aff = post_mix(x, o_gla, o_ssd, o_swa, dil_o, dil_l, p['w_out'][l].astype(BF16),
                       row(p['norm_cross'][l]), p['w_cq'][l].astype(BF16), k_mem, v_mem,
                       p['w_co'][l].astype(BF16), row(p['norm_ffn'][l]), p['w_router'][l].astype(F32).T,
                       bounds)

    cap = EC_FACTOR * (n // n_groups) // N_EXPERTS
    sel, pos, base = select_tokens(aff, n_groups, cap)
    idx_t, gate_t = compact_slots(aff, sel, pos, base, n_groups, cap)
    nrow = cap // 128
    idx_rows = jnp.moveaxis(idx_t[:, :, :nrow], 0, 1).reshape(N_EXPERTS, n_groups * cap)
    gate_rows = jnp.moveaxis(gate_t[:, :, :nrow], 0, 1).reshape(N_EXPERTS, n_groups * cap)
    ye = expert_ffn(x2, idx_rows, gate_rows, row(p['norm_ffn'][l]),
                    p['w_e_gate'][l].astype(BF16), p['w_e_up'][l].astype(BF16),
                    p['w_e_down'][l].astype(BF16))
    return combine(x2, ye, base, final_gain, n_groups, cap, final=(l == DEPTH - 1))


def _trunk(x_groups, mem_groups, p):
    xs, bounds, off = [], [0], 0
    for xg in x_groups:
        b, t, _ = xg.shape
        xs.append(xg.reshape(b * t, D_MODEL))
        for _ in range(b):
            off += t
            bounds.append(off)
    x = jnp.concatenate(xs, axis=0).astype(F32)
    mem = jnp.concatenate(mem_groups, axis=0).astype(F32)
    final_gain = p['final_norm'].astype(F32)[None, :]
    for l in range(DEPTH):
        x = _layer(x, mem, p, l, tuple(bounds), len(x_groups), final_gain)
    outs, off = [], 0
    for xg in x_groups:
        b, t, _ = xg.shape
        outs.append(x[off:off + b * t].reshape(b, t, D_MODEL).astype(xg.dtype))
        off += b * t
    return tuple(outs)


def kernel(x_prompt, x_sample, mem_prompt, mem_sample, norm_mix, w_in, gla_w_gate_up, gla_b_gate, gla_norm, ssd_conv_w, ssd_conv_b, ssd_dt_bias, ssd_a_log, ssd_d, ssd_norm, swa_sink, w_out, norm_cross, norm_mem, w_cq, w_ckv, w_co, norm_ffn, w_router, w_e_gate, w_e_up, w_e_down, final_norm):
    p = dict(norm_mix=norm_mix, w_in=w_in, gla_w_gate_up=gla_w_gate_up, gla_b_gate=gla_b_gate,
             gla_norm=gla_norm, ssd_conv_w=ssd_conv_w, ssd_conv_b=ssd_conv_b, ssd_dt_bias=ssd_dt_bias,
             ssd_a_log=ssd_a_log, ssd_d=ssd_d, ssd_norm=ssd_norm, swa_sink=swa_sink, w_out=w_out,
             norm_cross=norm_cross, norm_mem=norm_mem, w_cq=w_cq, w_ckv=w_ckv, w_co=w_co,
             norm_ffn=norm_ffn, w_router=w_router, w_e_gate=w_e_gate, w_e_up=w_e_up,
             w_e_down=w_e_down, final_norm=final_norm)
    return _trunk([x_prompt, x_sample], [mem_prompt, mem_sample], p)
```
